```python
import math
import jax, jax.numpy as jnp
from jax import lax
import numpy as np

D_MODEL = 1024
BATCH = 8
SEQ = 8192
DEPTH = 2

N_A = DEPTH // 2
N_B = DEPTH - N_A
D_FF = 2816
N_NORMS = 6
D_RNN = D_MODEL
N_RNN_BLOCKS = 16
RNN_BW = D_RNN // N_RNN_BLOCKS
CONV_W = 4
C_RGLRU = 8.0
HEAD_DIM = 64
N_HEADS = D_MODEL // HEAD_DIM
N_KV_HEADS = 4
GROUP = N_HEADS // N_KV_HEADS
WINDOW = 128
BLK = 128
EPS = 1e-6
NEG = -1e30

kernel_name = "yoco_rglru_swa_sinks_macaron"


def rms_norm(x, g):
    xf = x.astype(jnp.float32)
    y = xf * lax.rsqrt(jnp.mean(xf * xf, axis=-1, keepdims=True) + EPS)
    return (y * g.astype(jnp.float32)).astype(x.dtype)


def swiglu(x, w_gate, w_up, w_down):
    return (jax.nn.silu(x @ w_gate) * (x @ w_up)) @ w_down


def causal_depthwise_conv(x, w, b):
    s = x.shape[1]
    xp = jnp.pad(x, ((0, 0), (CONV_W - 1, 0), (0, 0)))
    y = b
    for k in range(CONV_W):
        y = y + xp[:, k:k + s] * w[k]
    return y


def block_diag_linear(x, w, b):
    bsz, s, _ = x.shape
    xb = x.reshape(bsz, s, N_RNN_BLOCKS, RNN_BW)
    return jnp.einsum('bshi,hij->bshj', xb, w).reshape(bsz, s, D_RNN) + b


def _lin_rec_combine(c1, c2):
    a1, b1 = c1
    a2, b2 = c2
    return a1 * a2, a2 * b1 + b2


def rg_lru(x, w_ga, b_ga, w_gx, b_gx, lam):
    r = jax.nn.sigmoid(block_diag_linear(x, w_ga, b_ga).astype(jnp.float32))
    i = jax.nn.sigmoid(block_diag_linear(x, w_gx, b_gx).astype(jnp.float32))
    log_a = -C_RGLRU * r * jax.nn.softplus(-lam.astype(jnp.float32))
    a = jnp.exp(log_a)
    mult = jnp.sqrt(-jnp.expm1(2.0 * log_a))
    u = mult * i * x.astype(jnp.float32)
    _, h = lax.associative_scan(_lin_rec_combine, (a, u), axis=1)
    return h.astype(x.dtype)


def recurrent_block(x, w_in, conv_w, conv_b, w_ga, b_ga, w_gx, b_gx, lam, w_out):
    u = x @ w_in
    xr, gate = u[..., :D_RNN], u[..., D_RNN:]
    xr = causal_depthwise_conv(xr, conv_w, conv_b)
    h = rg_lru(xr, w_ga, b_ga, w_gx, b_gx, lam)
    return (h * jax.nn.gelu(gate)) @ w_out


def shared_kv(x, kv_norm, w_kv):
    bsz, s, _ = x.shape
    kv = rms_norm(x, kv_norm) @ w_kv
    k = kv[..., :N_KV_HEADS * HEAD_DIM].reshape(bsz, s, N_KV_HEADS, HEAD_DIM)
    v = kv[..., N_KV_HEADS * HEAD_DIM:].reshape(bsz, s, N_KV_HEADS, HEAD_DIM)
    return k, v


def _band(t):
    bsz, s = t.shape[0], t.shape[1]
    nb = s // BLK
    tp = jnp.pad(t, ((0, 0), (BLK, 0), (0, 0), (0, 0)))
    tb = tp.reshape(bsz, nb + 1, BLK, N_KV_HEADS, HEAD_DIM)
    return jnp.concatenate([tb[:, :-1], tb[:, 1:]], axis=2)


def swa_sinks_attention(x, w_q, sinks, k, v, w_o):
    bsz, s, _ = x.shape
    nb = s // BLK
    q = (x @ w_q).reshape(bsz, nb, BLK, N_KV_HEADS, GROUP, HEAD_DIM)
    kb, vb = _band(k), _band(v)
    scores = jnp.einsum('bnqkgd,bnjkd->bnkgqj', q, kb).astype(jnp.float32) * (HEAD_DIM ** -0.5)
    qi = jnp.arange(BLK)[:, None]
    kj = jnp.arange(2 * BLK)[None, :]
    delta = qi + BLK - kj
    blk = jnp.arange(nb)[:, None, None]
    valid = (delta >= 0) & (delta < WINDOW) & (blk * BLK + kj - BLK >= 0)
    scores = jnp.where(valid[None, :, None, None], scores, NEG)
    sink = sinks.astype(jnp.float32).reshape(1, 1, N_KV_HEADS, GROUP, 1, 1)
    m = jnp.maximum(jnp.max(scores, axis=-1, keepdims=True), sink)
    p = jnp.exp(scores - m)
    probs = p / (jnp.sum(p, axis=-1, keepdims=True) + jnp.exp(sink - m))
    o = jnp.einsum('bnkgqj,bnjkd->bnqkgd', probs.astype(vb.dtype), vb)
    return o.reshape(bsz, s, N_HEADS * HEAD_DIM) @ w_o


def setup_inputs(seed: int = 0) -> dict:
    key = jax.random.key(seed)
    ks = jax.random.split(key, 24)
    f32 = jnp.float32

    def nrm(k, shape, fan_in):
        return jax.random.normal(k, shape, f32) * (fan_in ** -0.5)

    x = jax.random.normal(ks[0], (BATCH, SEQ, D_MODEL), f32)
    norms = 1.0 + 0.05 * jax.random.normal(ks[1], (DEPTH, N_NORMS, D_MODEL), f32)
    ffn_w_gate = nrm(ks[2], (DEPTH, 2, D_MODEL, D_FF), D_MODEL)
    ffn_w_up = nrm(ks[3], (DEPTH, 2, D_MODEL, D_FF), D_MODEL)
    ffn_w_down = nrm(ks[4], (DEPTH, 2, D_FF, D_MODEL), D_FF)
    a_w_in = nrm(ks[5], (N_A, D_MODEL, 2 * D_RNN), D_MODEL)
    a_conv_w = nrm(ks[6], (N_A, CONV_W, D_RNN), CONV_W)
    a_conv_b = 0.02 * jax.random.normal(ks[7], (N_A, D_RNN), f32)
    a_gate_a_w = nrm(ks[8], (N_A, N_RNN_BLOCKS, RNN_BW, RNN_BW), RNN_BW)
    a_gate_a_b = 0.02 * jax.random.normal(ks[9], (N_A, D_RNN), f32)
    a_gate_x_w = nrm(ks[10], (N_A, N_RNN_BLOCKS, RNN_BW, RNN_BW), RNN_BW)
    a_gate_x_b = 0.02 * jax.random.normal(ks[11], (N_A, D_RNN), f32)
    a_base = jax.random.uniform(ks[12], (N_A, D_RNN), f32, 0.9, 0.999)
    a_lambda = jnp.log(a_base) - jnp.log1p(-a_base)
    a_w_out = nrm(ks[13], (N_A, D_RNN, D_MODEL), D_RNN)
    kv_norm = 1.0 + 0.05 * jax.random.normal(ks[14], (D_MODEL,), f32)
    w_kv = nrm(ks[15], (D_MODEL, 2 * N_KV_HEADS * HEAD_DIM), D_MODEL)
    b_w_q = nrm(ks[16], (N_B, D_MODEL, N_HEADS * HEAD_DIM), D_MODEL)
    b_sinks = jax.random.normal(ks[17], (N_B, N_HEADS), f32)
    b_w_o = nrm(ks[18], (N_B, N_HEADS * HEAD_DIM, D_MODEL), N_HEADS * HEAD_DIM)
    return {"x": x, "norms": norms, "ffn_w_gate": ffn_w_gate, "ffn_w_up": ffn_w_up,
            "ffn_w_down": ffn_w_down, "a_w_in": a_w_in, "a_conv_w": a_conv_w,
            "a_conv_b": a_conv_b, "a_gate_a_w": a_gate_a_w, "a_gate_a_b": a_gate_a_b,
            "a_gate_x_w": a_gate_x_w, "a_gate_x_b": a_gate_x_b, "a_lambda": a_lambda,
            "a_w_out": a_w_out, "kv_norm": kv_norm, "w_kv": w_kv, "b_w_q": b_w_q,
            "b_sinks": b_sinks, "b_w_o": b_w_o}


def reference(x, norms, ffn_w_gate, ffn_w_up, ffn_w_down, a_w_in, a_conv_w, a_conv_b,
              a_gate_a_w, a_gate_a_b, a_gate_x_w, a_gate_x_b, a_lambda, a_w_out,
              kv_norm, w_kv, b_w_q, b_sinks, b_w_o):
    k_sh = v_sh = None
    for l in range(DEPTH):
        g = norms[l]
        f = swiglu(rms_norm(x, g[0]), ffn_w_gate[l, 0], ffn_w_up[l, 0], ffn_w_down[l, 0])
        x = x + 0.5 * rms_norm(f, g[1])
        h = rms_norm(x, g[2])
        if l < N_A:
            mix = recurrent_block(h, a_w_in[l], a_conv_w[l], a_conv_b[l],
                                  a_gate_a_w[l], a_gate_a_b[l], a_gate_x_w[l], a_gate_x_b[l],
                                  a_lambda[l], a_w_out[l])
        else:
            j = l - N_A
            mix = swa_sinks_attention(h, b_w_q[j], b_sinks[j], k_sh, v_sh, b_w_o[j])
        x = x + rms_norm(mix, g[3])
        f = swiglu(rms_norm(x, g[4]), ffn_w_gate[l, 1], ffn_w_up[l, 1], ffn_w_down[l, 1])
        x = x + 0.5 * rms_norm(f, g[5])
        if l == N_A - 1:
            k_sh, v_sh = shared_kv(x, kv_norm, w_kv)
    return x
```

```python
import functools

import jax
import jax.numpy as jnp
from jax import lax
from jax.experimental import pallas as pl
from jax.experimental.pallas import tpu as pltpu

D_MODEL = 1024
D_FF = 2816
D_RNN = 1024
N_RNN_BLOCKS = 16
RNN_BW = 64
CONV_W = 4
C_RGLRU = 8.0
HEAD_DIM = 64
N_HEADS = 16
N_KV_HEADS = 4
GROUP = 4
WINDOW = 128
BLK = 128
EPS = 1e-6
NEG = -1e30

F32 = jnp.float32
BF16 = jnp.bfloat16

VMEM_LIMIT_BYTES = 56 * 1024 * 1024
GATE_GROUP = 256
N_GATE_GROUPS = D_RNN // GATE_GROUP

FFN_TILE = 512
FFN_CHUNKS = (1024, 1024, 768)
REC_TILE = 256
ATT_TILE = 512


def _rms(x, g):
    ms = jnp.mean(x * x, axis=-1, keepdims=True)
    return x * lax.rsqrt(ms + EPS) * g


def _resident(shape):
    nd = len(shape)
    return pl.BlockSpec(shape, lambda *_: (0,) * nd, pipeline_mode=pl.Buffered(1))


def _ffn_body(x_ref, gpre_ref, gpost_ref, wg_ref, wu_ref, wd_ref, *rest, has_proj):
    if has_proj:
        ge_ref, we_ref, o_ref, p_ref = rest
    else:
        (o_ref,) = rest
    x = x_ref[...]
    xn = _rms(x, gpre_ref[...]).astype(BF16)
    acc = None
    off = 0
    for c in FFN_CHUNKS:
        g = jnp.dot(xn, wg_ref[:, off:off + c], preferred_element_type=F32)
        u = jnp.dot(xn, wu_ref[:, off:off + c], preferred_element_type=F32)
        h = (g * jax.nn.sigmoid(g) * u).astype(BF16)
        d = jnp.dot(h, wd_ref[off:off + c, :], preferred_element_type=F32)
        acc = d if acc is None else acc + d
        off += c
    y = x + 0.5 * _rms(acc, gpost_ref[...])
    o_ref[...] = y
    if has_proj:
        e = _rms(y, ge_ref[...]).astype(BF16)
        p_ref[...] = jnp.dot(e, we_ref[...], preferred_element_type=F32).astype(p_ref.dtype)


def _ffn(x2, gpre, gpost, wg, wu, wd, proj=None):
    n = x2.shape[0]
    assert n % FFN_TILE == 0 and sum(FFN_CHUNKS) == D_FF
    row = pl.BlockSpec((FFN_TILE, D_MODEL), lambda i: (i, 0))
    in_specs = [row, _resident((1, D_MODEL)), _resident((1, D_MODEL)),
                _resident(wg.shape), _resident(wu.shape), _resident(wd.shape)]
    args = [x2, gpre, gpost, wg, wu, wd]
    out_shape = [jax.ShapeDtypeStruct((n, D_MODEL), F32)]
    out_specs = [row]
    if proj is not None:
        ge, we = proj
        in_specs += [_resident((1, D_MODEL)), _resident(we.shape)]
        args += [ge, we]
        out_shape.append(jax.ShapeDtypeStruct((n, we.shape[1]), BF16))
        out_specs.append(pl.BlockSpec((FFN_TILE, we.shape[1]), lambda i: (i, 0)))
    res = pl.pallas_call(
        functools.partial(_ffn_body, has_proj=proj is not None),
        grid=(n // FFN_TILE,),
        in_specs=in_specs,
        out_specs=out_specs,
        out_shape=out_shape,
        compiler_params=pltpu.CompilerParams(
            dimension_semantics=("arbitrary",), vmem_limit_bytes=VMEM_LIMIT_BYTES),
        name="ffn_proj" if proj is not None else "ffn",
    )(*args)
    return res if proj is not None else res[0]


def _gelu_tanh(x):
    c = 0.7978845608028654
    return 0.5 * x * (1.0 + jnp.tanh(c * (x + 0.044715 * (x * x * x))))


def _rec_body(x_ref, g2_ref, g3_ref, win_ref, cw_ref, cb_ref, wgate_ref, bga_ref, bgx_ref,
              lam_ref, wout_ref, o_ref, ext_ref, a_ref, b_ref, carry_ref):
    ts = REC_TILE

    @pl.when(pl.program_id(1) == 0)
    def _():
        ext_ref[0:8, :] = jnp.zeros((8, D_RNN), F32)
        carry_ref[...] = jnp.zeros((1, D_RNN), F32)

    x = x_ref[0]
    hn = _rms(x, g2_ref[...]).astype(BF16)
    u = jnp.dot(hn, win_ref[...], preferred_element_type=F32)
    xr = u[:, :D_RNN]
    gate = u[:, D_RNN:]

    ext_ref[8:8 + ts, :] = xr
    conv = cb_ref[...] + cw_ref[3:4, :] * xr
    for k in range(CONV_W - 1):
        shift = CONV_W - 1 - k
        conv = conv + cw_ref[k:k + 1, :] * ext_ref[8 - shift:8 - shift + ts, :]
    ext_ref[0:8, :] = ext_ref[ts:ts + 8, :]

    xc = conv.astype(BF16)
    ga_parts, gx_parts = [], []
    for c in range(N_GATE_GROUPS):
        gg = jnp.dot(xc[:, c * GATE_GROUP:(c + 1) * GATE_GROUP], wgate_ref[c],
                     preferred_element_type=F32)
        ga_parts.append(gg[:, :GATE_GROUP])
        gx_parts.append(gg[:, GATE_GROUP:])
    r = jax.nn.sigmoid(jnp.concatenate(ga_parts, axis=-1) + bga_ref[...])
    i = jax.nn.sigmoid(jnp.concatenate(gx_parts, axis=-1) + bgx_ref[...])

    z = -lam_ref[...]
    softplus = jnp.maximum(z, 0.0) + jnp.log1p(jnp.exp(-jnp.abs(z)))
    log_a = (-C_RGLRU * r) * softplus
    a = jnp.exp(log_a)
    mult = jnp.sqrt(1.0 - a * a)
    a_ref[...] = a
    b_ref[...] = mult * i * conv

    def step(t, c):
        c = a_ref[pl.ds(t, 1), :] * c + b_ref[pl.ds(t, 1), :]
        b_ref[pl.ds(t, 1), :] = c
        return c

    carry_ref[...] = lax.fori_loop(0, ts, step, carry_ref[...], unroll=8)

    y = (b_ref[...] * _gelu_tanh(gate)).astype(BF16)
    out = jnp.dot(y, wout_ref[...], preferred_element_type=F32)
    o_ref[0] = x + _rms(out, g3_ref[...])


def _recurrent(x, g2, g3, win, cw, cb, wgate, bga, bgx, lam, wout):
    b, s, _ = x.shape
    assert s % REC_TILE == 0
    row = pl.BlockSpec((1, REC_TILE, D_MODEL), lambda bi, si: (bi, si, 0))
    vec = _resident((1, D_MODEL))
    return pl.pallas_call(
        _rec_body,
        grid=(b, s // REC_TILE),
        in_specs=[row, vec, vec, _resident(win.shape), _resident(cw.shape), vec,
                  _resident(wgate.shape), vec, vec, vec, _resident(wout.shape)],
        out_specs=row,
        out_shape=jax.ShapeDtypeStruct(x.shape, F32),
        scratch_shapes=[
            pltpu.VMEM((REC_TILE + 8, D_RNN), F32),
            pltpu.VMEM((REC_TILE, D_RNN), F32),
            pltpu.VMEM((REC_TILE, D_RNN), F32),
            pltpu.VMEM((1, D_RNN), F32),
        ],
        compiler_params=pltpu.CompilerParams(
            dimension_semantics=("arbitrary", "arbitrary"), vmem_limit_bytes=VMEM_LIMIT_BYTES),
        name="rglru_block",
    )(x, g2, g3, win, cw, cb, wgate, bga, bgx, lam, wout)


def _att_body(sinks_ref, x_ref, kvp_ref, kvc_ref, g2_ref, g3_ref, wq_ref, wo_ref, o_ref):
    nblk = ATT_TILE // BLK
    x = x_ref[0]
    hn = _rms(x, g2_ref[...]).astype(BF16)
    q = jnp.dot(hn, wq_ref[...], preferred_element_type=F32)
    q = (q * (HEAD_DIM ** -0.5)).astype(BF16)
    kv = jnp.concatenate([kvp_ref[0], kvc_ref[0]], axis=0)

    lane = lax.broadcasted_iota(jnp.int32, (BLK, 2 * HEAD_DIM), 1)
    lo = lane < HEAD_DIM
    qi = lax.broadcasted_iota(jnp.int32, (GROUP * BLK, 2 * BLK), 0) % BLK
    kj = lax.broadcasted_iota(jnp.int32, (GROUP * BLK, 2 * BLK), 1)
    delta = qi + BLK - kj
    band = (delta >= 0) & (delta < WINDOW)
    first_tile = pl.program_id(1) == 0
    zero = jnp.zeros((BLK, 2 * HEAD_DIM), BF16)

    o_blocks = []
    for j in range(nblk):
        valid = band
        if j == 0:
            valid = band & ((kj >= BLK) | jnp.logical_not(first_tile))
        qj = q[j * BLK:(j + 1) * BLK, :]
        o_cols = []
        for kh in range(N_KV_HEADS):
            kk = kv[j * BLK:(j + 2) * BLK, kh * 128:(kh + 1) * 128]
            vv = kv[j * BLK:(j + 2) * BLK, 512 + kh * 128:512 + (kh + 1) * 128]
            q0 = qj[:, kh * 256:kh * 256 + 128]
            q1 = qj[:, kh * 256 + 128:(kh + 1) * 256]
            lhs = jnp.concatenate([jnp.where(lo, q0, zero), jnp.where(lo, zero, q0),
                                   jnp.where(lo, q1, zero), jnp.where(lo, zero, q1)], axis=0)
            s = lax.dot_general(lhs, kk, (((1,), (1,)), ((), ())),
                                preferred_element_type=F32)
            s = jnp.where(valid, s, NEG)
            sink = jnp.concatenate(
                [jnp.full((BLK, 1), sinks_ref[kh * GROUP + g], F32) for g in range(GROUP)], axis=0)
            m = jnp.maximum(jnp.max(s, axis=-1, keepdims=True), sink)
            p = jnp.exp(s - m)
            den = jnp.sum(p, axis=-1, keepdims=True) + jnp.exp(sink - m)
            o = jnp.dot(p.astype(BF16), vv, preferred_element_type=F32) / den
            o_cols.append(jnp.where(lo, o[0:BLK], o[BLK:2 * BLK]))
            o_cols.append(jnp.where(lo, o[2 * BLK:3 * BLK], o[3 * BLK:4 * BLK]))
        o_blocks.append(jnp.concatenate(o_cols, axis=-1).astype(BF16))
    o_all = jnp.concatenate(o_blocks, axis=0)
    out = jnp.dot(o_all, wo_ref[...], preferred_element_type=F32)
    o_ref[0] = x + _rms(out, g3_ref[...])


def _attention(x, kvd, sinks, g2, g3, wq, wo):
    b, s, _ = x.shape
    assert s % ATT_TILE == 0 and ATT_TILE % BLK == 0
    ratio = ATT_TILE // BLK
    row = pl.BlockSpec((1, ATT_TILE, D_MODEL), lambda bi, si, *_: (bi, si, 0))
    vec = pl.BlockSpec((1, D_MODEL), lambda *_: (0, 0), pipeline_mode=pl.Buffered(1))
    mat = pl.BlockSpec((D_MODEL, D_MODEL), lambda *_: (0, 0), pipeline_mode=pl.Buffered(1))
    kv_prev = pl.BlockSpec((1, BLK, 2 * 512),
                           lambda bi, si, *_: (bi, jnp.maximum(si * ratio - 1, 0), 0))
    kv_cur = pl.BlockSpec((1, ATT_TILE, 2 * 512), lambda bi, si, *_: (bi, si, 0))
    return pl.pallas_call(
        _att_body,
        grid_spec=pltpu.PrefetchScalarGridSpec(
            num_scalar_prefetch=1,
            grid=(b, s // ATT_TILE),
            in_specs=[row, kv_prev, kv_cur, vec, vec, mat, mat],
            out_specs=row,
        ),
        out_shape=jax.ShapeDtypeStruct(x.shape, F32),
        compiler_params=pltpu.CompilerParams(
            dimension_semantics=("arbitrary", "arbitrary"), vmem_limit_bytes=VMEM_LIMIT_BYTES),
        name="swa_block",
    )(sinks, x, kvd, kvd, g2, g3, wq, wo)


def _block_diag_groups(w):
    per = GATE_GROUP // RNN_BW
    w4 = w.reshape(N_GATE_GROUPS, per, RNN_BW, RNN_BW)
    eye = jnp.eye(per, dtype=w.dtype)
    return jnp.einsum("chij,hk->chikj", w4, eye).reshape(N_GATE_GROUPS, GATE_GROUP, GATE_GROUP)


def _dup_heads(w):
    w3 = w.reshape(w.shape[0], N_KV_HEADS, HEAD_DIM)
    return jnp.concatenate([w3, w3], axis=-1).reshape(w.shape[0], N_KV_HEADS * 2 * HEAD_DIM)


def kernel(x, norms, ffn_w_gate, ffn_w_up, ffn_w_down, a_w_in, a_conv_w, a_conv_b, a_gate_a_w,
           a_gate_a_b, a_gate_x_w, a_gate_x_b, a_lambda, a_w_out, kv_norm, w_kv, b_w_q, b_sinks,
           b_w_o):
    bsz, seq, d = x.shape
    depth = norms.shape[0]
    n_a = depth // 2
    nk = N_KV_HEADS * HEAD_DIM
    row = lambda v: v.reshape(1, -1).astype(F32)

    wg = ffn_w_gate.astype(BF16)
    wu = ffn_w_up.astype(BF16)
    wd = ffn_w_down.astype(BF16)
    w_kvd = jnp.concatenate([_dup_heads(w_kv[:, :nk]), _dup_heads(w_kv[:, nk:])], axis=-1).astype(BF16)

    kvd = None
    for l in range(depth):
        g = norms[l]
        x = _ffn(x.reshape(bsz * seq, d), row(g[0]), row(g[1]), wg[l, 0], wu[l, 0], wd[l, 0])
        x = x.reshape(bsz, seq, d)
        if l < n_a:
            wgate = jnp.concatenate([_block_diag_groups(a_gate_a_w[l]),
                                     _block_diag_groups(a_gate_x_w[l])], axis=-1).astype(BF16)
            x = _recurrent(x, row(g[2]), row(g[3]), a_w_in[l].astype(BF16), a_conv_w[l],
                           row(a_conv_b[l]), wgate, row(a_gate_a_b[l]), row(a_gate_x_b[l]),
                           row(a_lambda[l]), a_w_out[l].astype(BF16))
        else:
            j = l - n_a
            x = _attention(x, kvd, b_sinks[j].astype(F32), row(g[2]), row(g[3]),
                           b_w_q[j].astype(BF16), b_w_o[j].astype(BF16))
        x2 = x.reshape(bsz * seq, d)
        if l == n_a - 1:
            x2, kvd = _ffn(x2, row(g[4]), row(g[5]), wg[l, 1], wu[l, 1], wd[l, 1],
                           proj=(row(kv_norm), w_kvd))
            kvd = kvd.reshape(bsz, seq, -1)
        else:
            x2 = _ffn(x2, row(g[4]), row(g[5]), wg[l, 1], wu[l, 1], wd[l, 1])
        x = x2.reshape(bsz, seq, d)
    return x
```
